```python
import math
import jax, jax.numpy as jnp
from jax import lax
import numpy as np

D_MODEL = 2048
BATCH = 2
SEQ = 8192
DEPTH = 4

MEM_LEN = 256
ATTN_HEAD_DIM = 64
N_Q_HEADS = 16
N_KV_HEADS = 2
WINDOW = 128
ROPE_THETA = 10000.0
ATTN_WIDTH = N_Q_HEADS * ATTN_HEAD_DIM
KV_WIDTH = N_KV_HEADS * ATTN_HEAD_DIM
N_RET_HEADS = 4
RET_QK_DIM = 128
RET_V_DIM = 256
RET_CHUNK = 128
RET_THETA = 10000.0
RET_QK_WIDTH = N_RET_HEADS * RET_QK_DIM
RET_V_WIDTH = N_RET_HEADS * RET_V_DIM
MIX_WIDTH = ATTN_WIDTH + RET_V_WIDTH
IN_WIDTH = ATTN_WIDTH + 2 * KV_WIDTH + 2 * RET_QK_WIDTH + 2 * RET_V_WIDTH
N_XATTN_HEADS = 4
XATTN_HEAD_DIM = D_MODEL // N_XATTN_HEADS
D_FF = 5632
N_EXPERTS = 8
TOP_K = 2
MOE_BLOCK = 256
LN_EPS = 1e-5
GN_EPS = 1e-6
DEEPNORM_ALPHA = (2 * DEPTH) ** 0.25
DEEPNORM_BETA = (8 * DEPTH) ** -0.25

kernel_name = 'hymba_swa_retention_deepnorm_moe_trunk'


def layer_norm(x, g, b):
    xf = x.astype(jnp.float32)
    mu = jnp.mean(xf, axis=-1, keepdims=True)
    var = jnp.mean(jnp.square(xf - mu), axis=-1, keepdims=True)
    y = (xf - mu) * lax.rsqrt(var + LN_EPS)
    return (y * g.astype(jnp.float32) + b.astype(jnp.float32)).astype(x.dtype)


def rope_half(x, positions):
    dh = x.shape[-1]
    inv = ROPE_THETA ** (-jnp.arange(0, dh // 2, dtype=jnp.float32) * 2.0 / dh)
    ang = positions.astype(jnp.float32)[..., None] * inv
    cos = jnp.cos(ang)[:, :, None, :].astype(x.dtype)
    sin = jnp.sin(ang)[:, :, None, :].astype(x.dtype)
    x1, x2 = x[..., : dh // 2], x[..., dh // 2:]
    return jnp.concatenate([x1 * cos - x2 * sin, x2 * cos + x1 * sin], axis=-1)


def retnet_rotate(x, positions):
    dk = x.shape[-1]
    inv = 1.0 / (RET_THETA ** jnp.linspace(0.0, 1.0, dk // 2, dtype=jnp.float32))
    ang = positions.astype(jnp.float32)[..., None] * inv
    cos = jnp.cos(ang)[:, :, None, :]
    sin = jnp.sin(ang)[:, :, None, :]
    xr = x.reshape(x.shape[:-1] + (dk // 2, 2))
    xe, xo = xr[..., 0], xr[..., 1]
    return jnp.stack([xe * cos - xo * sin, xo * cos + xe * sin], axis=-1).reshape(x.shape)


def sliding_window_sink_attention(q, k, v, sinks):
    B, S, Hq, Dh = q.shape
    Hkv = k.shape[2]
    G = Hq // Hkv
    W = WINDOW
    nb = S // W
    qb = q.reshape(B, nb, W, Hkv, G, Dh)

    def band(t):
        tb = t.reshape(B, nb, W, Hkv, Dh)
        prev = jnp.pad(tb[:, :-1], ((0, 0), (1, 0), (0, 0), (0, 0), (0, 0)))
        return jnp.concatenate([prev, tb], axis=2)

    kk, vv = band(k), band(v)
    s = jnp.einsum('bnqhgd,bnkhd->bnhgqk', qb, kk).astype(jnp.float32) * (Dh ** -0.5)
    qi = jnp.arange(W)[:, None] + W
    kj = jnp.arange(2 * W)[None, :]
    diff = qi - kj
    local = (diff >= 0) & (diff < W)
    has_prev = (jnp.arange(nb) > 0)[:, None, None] | (kj >= W)[None]
    mask = local[None] & has_prev
    s = jnp.where(mask[None, :, None, None], s, -jnp.inf)
    sink = jnp.broadcast_to(sinks.astype(jnp.float32).reshape(1, 1, Hkv, G, 1, 1), s.shape[:-1] + (1,))
    p = jax.nn.softmax(jnp.concatenate([s, sink], axis=-1), axis=-1)[..., :-1]
    o = jnp.einsum('bnhgqk,bnkhd->bnqhgd', p.astype(v.dtype), vv)
    return o.reshape(B, S, Hq * Dh)


def chunkwise_retention(q, k, v):
    B, S, H, dk = q.shape
    dv = v.shape[-1]
    C = RET_CHUNK
    N = S // C
    lg = jnp.log(1.0 - 2.0 ** (-5.0 - jnp.arange(H, dtype=jnp.float32)))
    qc = (q * (dk ** -0.5)).reshape(B, N, C, H, dk)
    kc = k.reshape(B, N, C, H, dk)
    vc = v.reshape(B, N, C, H, dv)
    c = jnp.arange(C, dtype=jnp.float32)
    diff = c[:, None] - c[None, :]
    decay = jnp.where(diff >= 0, jnp.exp(lg[:, None, None] * jnp.maximum(diff, 0.0)), 0.0)
    scores = jnp.einsum('bnchd,bnshd->bnhcs', qc, kc) * decay
    intra = jnp.einsum('bnhcs,bnshe->bnche', scores, vc)
    zeta = jnp.exp(lg[:, None] * (C - 1.0 - c)[None])
    xi = jnp.exp(lg[:, None] * (c + 1.0)[None])
    g_chunk = jnp.exp(lg * C)
    kv = jnp.einsum('bnshd,hs,bnshe->bnhde', kc, zeta, vc)

    def step(state, kv_n):
        return g_chunk[None, :, None, None] * state + kv_n, state

    _, r_prev = lax.scan(step, jnp.zeros((B, H, dk, dv), jnp.float32), jnp.moveaxis(kv, 1, 0))
    r_prev = jnp.moveaxis(r_prev, 0, 1)
    inter = jnp.einsum('bnchd,bnhde->bnche', qc, r_prev) * xi.T[None, None, :, :, None]
    return (intra + inter).reshape(B, S, H, dv)


def head_group_norm(o):
    mu = jnp.mean(o, axis=-1, keepdims=True)
    var = jnp.mean(jnp.square(o - mu), axis=-1, keepdims=True)
    return (o - mu) * lax.rsqrt(var + GN_EPS)


def hybrid_mixer(h, positions, w_in, sinks, w_o):
    B, S, _ = h.shape
    proj = h @ w_in
    sizes = [ATTN_WIDTH, KV_WIDTH, KV_WIDTH, RET_QK_WIDTH, RET_QK_WIDTH, RET_V_WIDTH, RET_V_WIDTH]
    cuts = [int(c) for c in np.cumsum(sizes)[:-1]]
    aq, ak, av, rq, rk, rv, rg = jnp.split(proj, cuts, axis=-1)
    aq = rope_half(aq.reshape(B, S, N_Q_HEADS, ATTN_HEAD_DIM), positions)
    ak = rope_half(ak.reshape(B, S, N_KV_HEADS, ATTN_HEAD_DIM), positions)
    av = av.reshape(B, S, N_KV_HEADS, ATTN_HEAD_DIM)
    attn_out = sliding_window_sink_attention(aq, ak, av, sinks)
    rq = retnet_rotate(rq.reshape(B, S, N_RET_HEADS, RET_QK_DIM).astype(jnp.float32), positions)
    rk = retnet_rotate(rk.reshape(B, S, N_RET_HEADS, RET_QK_DIM).astype(jnp.float32), positions)
    rv = rv.reshape(B, S, N_RET_HEADS, RET_V_DIM).astype(jnp.float32)
    ret = head_group_norm(chunkwise_retention(rq, rk, rv)).reshape(B, S, RET_V_WIDTH).astype(h.dtype)
    ret_out = jax.nn.silu(rg) * ret
    return jnp.concatenate([attn_out, ret_out], axis=-1) @ w_o


def memory_cross_attention(h, mem, wq, wk, wv, wo):
    B, S, D = h.shape
    M = mem.shape[1]
    q = (h @ wq).reshape(B, S, N_XATTN_HEADS, XATTN_HEAD_DIM)
    k = (mem @ wk).reshape(B, M, N_XATTN_HEADS, XATTN_HEAD_DIM)
    v = (mem @ wv).reshape(B, M, N_XATTN_HEADS, XATTN_HEAD_DIM)
    s = jnp.einsum('bqhd,bkhd->bhqk', q, k).astype(jnp.float32) * (XATTN_HEAD_DIM ** -0.5)
    p = jax.nn.softmax(s, axis=-1).astype(v.dtype)
    o = jnp.einsum('bhqk,bkhd->bqhd', p, v).reshape(B, S, N_XATTN_HEADS * XATTN_HEAD_DIM)
    return o @ wo


def swiglu(h, w1, w3, w2):
    return (jax.nn.silu(h @ w1) * (h @ w3)) @ w2


def moe_swiglu(h, router, w1, w3, w2):
    B, S, D = h.shape
    T = B * S
    A = T * TOP_K
    xf = h.reshape(T, D)
    logits = (xf @ router).astype(jnp.float32)
    top_vals, top_idx = lax.top_k(logits, TOP_K)
    gates = jax.nn.softmax(top_vals, axis=-1).astype(h.dtype)
    e_flat = top_idx.reshape(A)
    tok_flat = jnp.repeat(jnp.arange(T, dtype=jnp.int32), TOP_K)
    gate_flat = gates.reshape(A)
    order = jnp.argsort(e_flat)
    s_e, s_tok, s_gate = e_flat[order], tok_flat[order], gate_flat[order]
    sizes = jnp.bincount(e_flat, length=N_EXPERTS)
    padded = ((sizes + MOE_BLOCK - 1) // MOE_BLOCK) * MOE_BLOCK
    start = jnp.cumsum(sizes) - sizes
    pend = jnp.cumsum(padded)
    pstart = pend - padded
    dest = pstart[s_e] + jnp.arange(A) - start[s_e]
    n_blocks = -(-A // MOE_BLOCK) + N_EXPERTS
    P = n_blocks * MOE_BLOCK
    row_tok = jnp.full((P,), T, jnp.int32).at[dest].set(s_tok)
    row_gate = jnp.zeros((P,), h.dtype).at[dest].set(s_gate)
    block_e = jnp.clip(jnp.searchsorted(pend, jnp.arange(n_blocks) * MOE_BLOCK, side='right'), 0, N_EXPERTS - 1)
    x_pad = jnp.concatenate([xf, jnp.zeros((1, D), xf.dtype)], axis=0)
    xr = x_pad[row_tok].reshape(n_blocks, MOE_BLOCK, D)

    def expert_block(args):
        xb, e = args
        return (jax.nn.silu(xb @ w1[e]) * (xb @ w3[e])) @ w2[e]

    yr = lax.map(expert_block, (xr, block_e)).reshape(P, D)
    out = jnp.zeros((T + 1, D), h.dtype).at[row_tok].add(yr * row_gate[:, None])[:T]
    return out.reshape(B, S, D)


def setup_inputs(seed: int = 0) -> dict:
    key = jax.random.key(seed)
    ks = jax.random.split(key, 24)
    n_dense = (DEPTH + 1) // 2
    n_moe = DEPTH // 2

    def nrm(k, shape, scale):
        return jax.random.normal(k, shape, jnp.float32) * scale

    x = nrm(ks[0], (BATCH, SEQ, D_MODEL), 1.0)
    mem = nrm(ks[1], (BATCH, MEM_LEN, D_MODEL), 1.0)
    positions = jnp.arange(SEQ, dtype=jnp.int32)[None, :] + jax.random.randint(ks[2], (BATCH, 1), 0, 4096, dtype=jnp.int32)
    ln_g = 1.0 + nrm(ks[3], (DEPTH, 3, D_MODEL), 0.02)
    ln_b = nrm(ks[4], (DEPTH, 3, D_MODEL), 0.02)
    w_in = nrm(ks[5], (DEPTH, D_MODEL, IN_WIDTH), D_MODEL ** -0.5)
    sinks = nrm(ks[6], (DEPTH, N_Q_HEADS), 0.5)
    w_o = nrm(ks[7], (DEPTH, MIX_WIDTH, D_MODEL), DEEPNORM_BETA * MIX_WIDTH ** -0.5)
    xq = nrm(ks[8], (DEPTH, D_MODEL, D_MODEL), D_MODEL ** -0.5)
    xk = nrm(ks[9], (DEPTH, D_MODEL, D_MODEL), D_MODEL ** -0.5)
    xv = nrm(ks[10], (DEPTH, D_MODEL, D_MODEL), D_MODEL ** -0.5)
    xo = nrm(ks[11], (DEPTH, D_MODEL, D_MODEL), DEEPNORM_BETA * D_MODEL ** -0.5)
    ffn_w1 = nrm(ks[12], (n_dense, D_MODEL, D_FF), D_MODEL ** -0.5)
    ffn_w3 = nrm(ks[13], (n_dense, D_MODEL, D_FF), D_MODEL ** -0.5)
    ffn_w2 = nrm(ks[14], (n_dense, D_FF, D_MODEL), DEEPNORM_BETA * D_FF ** -0.5)
    router = nrm(ks[15], (n_moe, D_MODEL, N_EXPERTS), D_MODEL ** -0.5)
    moe_w1 = nrm(ks[16], (n_moe, N_EXPERTS, D_MODEL, D_FF), D_MODEL ** -0.5)
    moe_w3 = nrm(ks[17], (n_moe, N_EXPERTS, D_MODEL, D_FF), D_MODEL ** -0.5)
    moe_w2 = nrm(ks[18], (n_moe, N_EXPERTS, D_FF, D_MODEL), DEEPNORM_BETA * D_FF ** -0.5)
    return {'x': x, 'mem': mem, 'positions': positions, 'ln_g': ln_g, 'ln_b': ln_b,
            'w_in': w_in, 'sinks': sinks, 'w_o': w_o, 'xq': xq, 'xk': xk, 'xv': xv, 'xo': xo,
            'ffn_w1': ffn_w1, 'ffn_w3': ffn_w3, 'ffn_w2': ffn_w2, 'router': router,
            'moe_w1': moe_w1, 'moe_w3': moe_w3, 'moe_w2': moe_w2}


def reference(x, mem, positions, ln_g, ln_b, w_in, sinks, w_o, xq, xk, xv, xo,
              ffn_w1, ffn_w3, ffn_w2, router, moe_w1, moe_w3, moe_w2):
    for l in range(DEPTH):
        x = layer_norm(DEEPNORM_ALPHA * x + hybrid_mixer(x, positions, w_in[l], sinks[l], w_o[l]),
                       ln_g[l, 0], ln_b[l, 0])
        x = layer_norm(DEEPNORM_ALPHA * x + memory_cross_attention(x, mem, xq[l], xk[l], xv[l], xo[l]),
                       ln_g[l, 1], ln_b[l, 1])
        if l % 2 == 0:
            f = swiglu(x, ffn_w1[l // 2], ffn_w3[l // 2], ffn_w2[l // 2])
        else:
            f = moe_swiglu(x, router[l // 2], moe_w1[l // 2], moe_w3[l // 2], moe_w2[l // 2])
        x = layer_norm(DEEPNORM_ALPHA * x + f, ln_g[l, 2], ln_b[l, 2])
    return x
```

```python
import functools

import numpy as np
import jax
import jax.numpy as jnp
from jax import lax
from jax.experimental import pallas as pl
from jax.experimental.pallas import tpu as pltpu

F32 = jnp.float32
BF16 = jnp.bfloat16

D_MODEL = 2048
DEPTH = 4
ATTN_HEAD_DIM = 64
N_Q_HEADS = 16
N_KV_HEADS = 2
WINDOW = 128
ROPE_THETA = 10000.0
ATTN_WIDTH = N_Q_HEADS * ATTN_HEAD_DIM
KV_WIDTH = N_KV_HEADS * ATTN_HEAD_DIM
N_RET_HEADS = 4
RET_QK_DIM = 128
RET_V_DIM = 256
RET_CHUNK = 128
RET_THETA = 10000.0
RET_QK_WIDTH = N_RET_HEADS * RET_QK_DIM
RET_V_WIDTH = N_RET_HEADS * RET_V_DIM
IN_WIDTH = ATTN_WIDTH + 2 * KV_WIDTH + 2 * RET_QK_WIDTH + 2 * RET_V_WIDTH
N_XATTN_HEADS = 4
XATTN_HEAD_DIM = D_MODEL // N_XATTN_HEADS
N_EXPERTS = 8
TOP_K = 2
LN_EPS = 1e-5
GN_EPS = 1e-6
DEEPNORM_ALPHA = (2 * DEPTH) ** 0.25

_C_Q = 0
_C_K = _C_Q + ATTN_WIDTH
_C_V = _C_K + KV_WIDTH
_C_RQ = _C_V + KV_WIDTH
_C_RK = _C_RQ + RET_QK_WIDTH
_C_RV = _C_RK + RET_QK_WIDTH
_C_RG = _C_RV + RET_V_WIDTH

LANES = 128
VMEM_PHYSICAL_BYTES = 64 << 20
VMEM_RESERVE_BYTES = 6 << 20

TM_PROJ = 512
TQ_ATTN = 512
RET_ROWS = 512
TM_OUT = 512
TM_XATTN = 256
TM_FFN = 512
TF_FFN = 512
TM_ROUTE = 512
MOE_TILE = 1024
MOE_SUB = 256
ROW_DMA_TILE = 256
NEG_BIG = -1e30


def _cparams(semantics, vmem_estimate_bytes):
    limit = min(int(vmem_estimate_bytes) + VMEM_RESERVE_BYTES, VMEM_PHYSICAL_BYTES - (4 << 20))
    return pltpu.CompilerParams(dimension_semantics=semantics, vmem_limit_bytes=limit)


def _resident(shape):
    return pl.BlockSpec(shape, lambda *_: (0,) * len(shape), pipeline_mode=pl.Buffered(1))


def _layer_norm(y, g, b):
    mu = jnp.mean(y, axis=-1, keepdims=True)
    d = y - mu
    var = jnp.mean(d * d, axis=-1, keepdims=True)
    return d * lax.rsqrt(var + LN_EPS) * g + b


def _silu(x):
    return x / (1.0 + jnp.exp(-x))


def _dot(a, b):
    return jnp.dot(a, b, preferred_element_type=F32)


def _dot_nt(a, b):
    return lax.dot_general(a, b, (((1,), (1,)), ((), ())), preferred_element_type=F32)


def _dot_tn(a, b):
    return lax.dot_general(a, b, (((0,), (0,)), ((), ())), preferred_element_type=F32)


def _in_proj_kernel(xb_ref, w_ref, ca_ref, sa_ref, cr_ref, sr_ref,
                    q_ref, k_ref, v_ref, rq_ref, rk_ref, rv_ref, rg_ref):
    xb = xb_ref[...]
    tm = xb.shape[0]
    lane = lax.broadcasted_iota(jnp.int32, (tm, LANES), 1)
    low_half = (lane % ATTN_HEAD_DIM) < (ATTN_HEAD_DIM // 2)
    ca, sa = ca_ref[...], sa_ref[...]
    cr, sr = cr_ref[...], sr_ref[...]

    def rope_pair(x):
        swapped = jnp.where(low_half, pltpu.roll(x, LANES - 32, 1), pltpu.roll(x, 32, 1))
        return x * ca + swapped * sa

    def ret_rotate(x):
        return x * cr + pltpu.roll(x, LANES // 2, 1) * sr

    for c in range(ATTN_WIDTH // LANES):
        lo = _C_Q + c * LANES
        y = _dot(xb, w_ref[:, lo:lo + LANES])
        q_ref[:, c * LANES:(c + 1) * LANES] = (rope_pair(y) * (ATTN_HEAD_DIM ** -0.5)).astype(BF16)
    k_ref[...] = rope_pair(_dot(xb, w_ref[:, _C_K:_C_K + KV_WIDTH])).astype(BF16)
    v_ref[...] = _dot(xb, w_ref[:, _C_V:_C_V + KV_WIDTH]).astype(BF16)
    for h in range(N_RET_HEADS):
        lo = _C_RQ + h * RET_QK_DIM
        y = ret_rotate(_dot(xb, w_ref[:, lo:lo + RET_QK_DIM]))
        rq_ref[:, h * RET_QK_DIM:(h + 1) * RET_QK_DIM] = (y * (RET_QK_DIM ** -0.5)).astype(BF16)
        lo = _C_RK + h * RET_QK_DIM
        rk_ref[:, h * RET_QK_DIM:(h + 1) * RET_QK_DIM] = ret_rotate(_dot(xb, w_ref[:, lo:lo + RET_QK_DIM]))
    for c in range(RET_V_WIDTH // 256):
        lo = _C_RV + c * 256
        rv_ref[:, c * 256:(c + 1) * 256] = _dot(xb, w_ref[:, lo:lo + 256]).astype(BF16)
        lo = _C_RG + c * 256
        rg_ref[:, c * 256:(c + 1) * 256] = _dot(xb, w_ref[:, lo:lo + 256])


def _in_proj(xb, w_in, ca, sa, cr, sr):
    t = xb.shape[0]
    tm = TM_PROJ
    row = lambda w: pl.BlockSpec((tm, w), lambda i: (i, 0))
    out_shapes = (
        jax.ShapeDtypeStruct((t, ATTN_WIDTH), BF16),
        jax.ShapeDtypeStruct((t, KV_WIDTH), BF16),
        jax.ShapeDtypeStruct((t, KV_WIDTH), BF16),
        jax.ShapeDtypeStruct((t, RET_QK_WIDTH), BF16),
        jax.ShapeDtypeStruct((t, RET_QK_WIDTH), F32),
        jax.ShapeDtypeStruct((t, RET_V_WIDTH), BF16),
        jax.ShapeDtypeStruct((t, RET_V_WIDTH), F32),
    )
    out_bytes = tm * (2 * ATTN_WIDTH + 4 * KV_WIDTH + 6 * RET_QK_WIDTH + 6 * RET_V_WIDTH)
    est = D_MODEL * IN_WIDTH * 2 + 2 * (tm * D_MODEL * 2 + 4 * tm * LANES * 4 + out_bytes) + (8 << 20)
    return pl.pallas_call(
        _in_proj_kernel,
        grid=(t // tm,),
        in_specs=[row(D_MODEL), _resident((D_MODEL, IN_WIDTH)),
                  row(LANES), row(LANES), row(LANES), row(LANES)],
        out_specs=(row(ATTN_WIDTH), row(KV_WIDTH), row(KV_WIDTH), row(RET_QK_WIDTH),
                   row(RET_QK_WIDTH), row(RET_V_WIDTH), row(RET_V_WIDTH)),
        out_shape=out_shapes,
        compiler_params=_cparams(("parallel",), est),
        name="in_proj",
    )(xb, w_in, ca, sa, cr, sr)


def _swa_kernel(sinks_ref, q_ref, kc_ref, vc_ref, kp_ref, vp_ref, o_ref, kb_ref, vb_ref):
    i = pl.program_id(1)
    w = WINDOW
    kb_ref[0:w, :] = kp_ref[...]
    kb_ref[w:, :] = kc_ref[...]
    vb_ref[0:w, :] = vp_ref[...]
    vb_ref[w:, :] = vc_ref[...]

    r2 = lax.broadcasted_iota(jnp.int32, (2 * w, 2 * w), 0) % w
    c2 = lax.broadcasted_iota(jnp.int32, (2 * w, 2 * w), 1)
    band = (c2 > r2) & (c2 <= r2 + w)
    row_q = lax.broadcasted_iota(jnp.int32, (2 * w, LANES), 0)
    lane_q = lax.broadcasted_iota(jnp.int32, (2 * w, LANES), 1)
    keep_q = (row_q < w) == (lane_q < ATTN_HEAD_DIM)
    row_s = lax.broadcasted_iota(jnp.int32, (2 * w, 1), 0)
    lane_o = lax.broadcasted_iota(jnp.int32, (w, LANES), 1)
    n_pairs = ATTN_WIDTH // LANES

    def sub_block(j, carry):
        r0 = pl.multiple_of(j * w, w)
        kband = kb_ref[pl.ds(r0, 2 * w), :]
        vband = vb_ref[pl.ds(r0, 2 * w), :]
        has_prev = jnp.logical_or(j > 0, i > 0)
        mask = band & jnp.logical_or(c2 >= w, has_prev)
        for c in range(n_pairs):
            qc = q_ref[pl.ds(r0, w), c * LANES:(c + 1) * LANES]
            q2 = jnp.concatenate([qc, qc], axis=0)
            q2 = jnp.where(keep_q, q2, jnp.zeros_like(q2))
            s = _dot_nt(q2, kband)
            s = jnp.where(mask, s, NEG_BIG)
            sink = jnp.where(row_s < w, sinks_ref[c], sinks_ref[n_pairs + c])
            m = jnp.maximum(jnp.max(s, axis=1, keepdims=True), sink)
            p = jnp.exp(s - m)
            denom = jnp.sum(p, axis=1, keepdims=True) + jnp.exp(sink - m)
            p = p * (1.0 / denom)
            pv = _dot(p.astype(BF16), vband)
            o = jnp.where(lane_o < ATTN_HEAD_DIM, pv[:w], pv[w:])
            o_ref[pl.ds(r0, w), c * LANES:(c + 1) * LANES] = o.astype(BF16)
        return carry

    lax.fori_loop(0, q_ref.shape[0] // w, sub_block, 0)


def _swa_attention(q, k, v, sinks, batch, seq):
    t = q.shape[0]
    tq = TQ_ATTN
    nq = seq // tq
    blocks_per_step = tq // WINDOW
    blocks_per_seq = seq // WINDOW
    cur = lambda wd: pl.BlockSpec((tq, wd), lambda b, i: (b * nq + i, 0))
    prev = pl.BlockSpec(
        (WINDOW, KV_WIDTH),
        lambda b, i: (jnp.maximum(b * blocks_per_seq + i * blocks_per_step - 1, 0), 0))
    est = 2 * (2 * tq * ATTN_WIDTH * 2 + 2 * tq * KV_WIDTH * 2) + (8 << 20)
    return pl.pallas_call(
        _swa_kernel,
        grid=(batch, nq),
        in_specs=[pl.BlockSpec(memory_space=pltpu.SMEM),
                  cur(ATTN_WIDTH), cur(KV_WIDTH), cur(KV_WIDTH), prev, prev],
        out_specs=cur(ATTN_WIDTH),
        out_shape=jax.ShapeDtypeStruct((t, ATTN_WIDTH), BF16),
        scratch_shapes=[pltpu.VMEM((tq + WINDOW, KV_WIDTH), BF16),
                        pltpu.VMEM((tq + WINDOW, KV_WIDTH), BF16)],
        compiler_params=_cparams(("parallel", "parallel"), est),
        name="swa_attention",
    )(sinks, q, k, v, k, v)


def _retention_kernel(gch_ref, rq_ref, rk_ref, rv_ref, rg_ref, decay_ref, zeta_ref, xi_ref,
                      o_ref, state_ref):
    h = pl.program_id(1)
    n = pl.program_id(2)
    c = RET_CHUNK

    @pl.when(n == 0)
    def _():
        state_ref[...] = jnp.zeros_like(state_ref)

    decay = decay_ref[...]
    zeta = zeta_ref[...]
    xi = xi_ref[...]
    g_chunk = gch_ref[h]
    for j in range(rq_ref.shape[0] // c):
        rows = slice(j * c, (j + 1) * c)
        q = rq_ref[rows, :]
        kf = rk_ref[rows, :]
        v = rv_ref[rows, :]
        scores = _dot_nt(q, kf.astype(BF16)) * decay
        intra = _dot(scores.astype(BF16), v)
        state = state_ref[...]
        inter = _dot(q, state.astype(BF16)) * xi
        kv = _dot_tn((kf * zeta).astype(BF16), v)
        state_ref[...] = g_chunk * state + kv
        o = intra + inter
        mu = jnp.mean(o, axis=-1, keepdims=True)
        d = o - mu
        var = jnp.mean(d * d, axis=-1, keepdims=True)
        normed = d * lax.rsqrt(var + GN_EPS)
        o_ref[rows, :] = (_silu(rg_ref[rows, :]) * normed).astype(BF16)


def _retention(rq, rk, rv, rg, decay, zeta, xi, g_chunk, batch, seq):
    t = rq.shape[0]
    r = RET_ROWS
    ns = seq // r
    qk = lambda: pl.BlockSpec((r, RET_QK_DIM), lambda b, h, n: (b * ns + n, h))
    vv = lambda: pl.BlockSpec((r, RET_V_DIM), lambda b, h, n: (b * ns + n, h))
    per_head = lambda wd: pl.BlockSpec((None, RET_CHUNK, wd), lambda b, h, n: (h, 0, 0))
    est = 2 * r * (RET_QK_DIM * 6 + RET_V_DIM * 8) + (8 << 20)
    return pl.pallas_call(
        _retention_kernel,
        grid=(batch, N_RET_HEADS, ns),
        in_specs=[pl.BlockSpec(memory_space=pltpu.SMEM),
                  qk(), qk(), vv(), vv(), per_head(RET_CHUNK), per_head(1), per_head(1)],
        out_specs=vv(),
        out_shape=jax.ShapeDtypeStruct((t, RET_V_WIDTH), BF16),
        scratch_shapes=[pltpu.VMEM((RET_QK_DIM, RET_V_DIM), F32)],
        compiler_params=_cparams(("parallel", "parallel", "arbitrary"), est),
        name="retention",
    )(g_chunk, rq, rk, rv, rg, decay, zeta, xi)


def _out_proj_kernel(a_ref, r_ref, x_ref, w_ref, g_ref, b_ref, xo_ref, xbo_ref):
    y = _dot(a_ref[...], w_ref[0:ATTN_WIDTH, :]) + _dot(r_ref[...], w_ref[ATTN_WIDTH:, :])
    z = _layer_norm(DEEPNORM_ALPHA * x_ref[...] + y, g_ref[...], b_ref[...])
    xo_ref[...] = z
    xbo_ref[...] = z.astype(BF16)


def _out_proj(a, r, x, w_o, g, b):
    t = x.shape[0]
    tm = TM_OUT
    row = lambda wd: pl.BlockSpec((tm, wd), lambda i: (i, 0))
    est = D_MODEL * D_MODEL * 2 + 2 * tm * (2 * 2 * ATTN_WIDTH + 10 * D_MODEL) + (8 << 20)
    return pl.pallas_call(
        _out_proj_kernel,
        grid=(t // tm,),
        in_specs=[row(ATTN_WIDTH), row(RET_V_WIDTH), row(D_MODEL), _resident((D_MODEL, D_MODEL)),
                  _resident((1, D_MODEL)), _resident((1, D_MODEL))],
        out_specs=(row(D_MODEL), row(D_MODEL)),
        out_shape=(jax.ShapeDtypeStruct((t, D_MODEL), F32), jax.ShapeDtypeStruct((t, D_MODEL), BF16)),
        compiler_params=_cparams(("parallel",), est),
        name="out_proj_ln",
    )(a, r, x, w_o, g, b)


def _matmul_kernel(a_ref, w_ref, o_ref):
    o_ref[...] = _dot(a_ref[...], w_ref[...]).astype(o_ref.dtype)


def _matmul_bf16(a, w, tn=512):
    m, kdim = a.shape
    n = w.shape[1]
    est = 2 * (m * kdim * 2 + kdim * tn * 2 + m * tn * 2) + (4 << 20)
    return pl.pallas_call(
        _matmul_kernel,
        grid=(n // tn,),
        in_specs=[pl.BlockSpec((m, kdim), lambda j: (0, 0)), pl.BlockSpec((kdim, tn), lambda j: (0, j))],
        out_specs=pl.BlockSpec((m, tn), lambda j: (0, j)),
        out_shape=jax.ShapeDtypeStruct((m, n), BF16),
        compiler_params=_cparams(("parallel",), est),
        name="mem_kv_proj",
    )(a, w)


def _xattn_kernel(xb_ref, x_ref, k_ref, v_ref, wq_ref, wo_ref, g_ref, b_ref, xo_ref, xbo_ref):
    q = _dot(xb_ref[...], wq_ref[...]).astype(BF16)
    scale = XATTN_HEAD_DIM ** -0.5
    heads = []
    for h in range(N_XATTN_HEADS):
        cols = slice(h * XATTN_HEAD_DIM, (h + 1) * XATTN_HEAD_DIM)
        s = _dot_nt(q[:, cols], k_ref[:, cols]) * scale
        m = jnp.max(s, axis=1, keepdims=True)
        p = jnp.exp(s - m)
        p = p * (1.0 / jnp.sum(p, axis=1, keepdims=True))
        heads.append(_dot(p.astype(BF16), v_ref[:, cols]).astype(BF16))
    o = jnp.concatenate(heads, axis=1)
    y = _dot(o, wo_ref[...])
    z = _layer_norm(DEEPNORM_ALPHA * x_ref[...] + y, g_ref[...], b_ref[...])
    xo_ref[...] = z
    xbo_ref[...] = z.astype(BF16)


def _xattn(xb, x, kmem, vmem, wq, wo, g, b, seq, mem_len):
    t = x.shape[0]
    tm = TM_XATTN
    row = lambda: pl.BlockSpec((tm, D_MODEL), lambda i: (i, 0))
    mem = lambda: pl.BlockSpec((mem_len, D_MODEL), lambda i: ((i * tm) // seq, 0))
    est = (2 * D_MODEL * D_MODEL * 2 + 2 * 2 * mem_len * D_MODEL * 2
           + 2 * tm * D_MODEL * 12 + 6 * tm * D_MODEL * 4 + (8 << 20))
    return pl.pallas_call(
        _xattn_kernel,
        grid=(t // tm,),
        in_specs=[row(), row(), mem(), mem(), _resident((D_MODEL, D_MODEL)), _resident((D_MODEL, D_MODEL)),
                  _resident((1, D_MODEL)), _resident((1, D_MODEL))],
        out_specs=(row(), row()),
        out_shape=(jax.ShapeDtypeStruct((t, D_MODEL), F32), jax.ShapeDtypeStruct((t, D_MODEL), BF16)),
        compiler_params=_cparams(("parallel",), est),
        name="mem_xattn_ln",
    )(xb, x, kmem, vmem, wq, wo, g, b)


def _ffn_kernel(xb_ref, x_ref, w1_ref, w3_ref, w2_ref, g_ref, b_ref, xo_ref, xbo_ref, acc_ref):
    f = pl.program_id(1)

    @pl.when(f == 0)
    def _():
        acc_ref[...] = jnp.zeros_like(acc_ref)

    xb = xb_ref[...]
    hidden = (_silu(_dot(xb, w1_ref[...])) * _dot(xb, w3_ref[...])).astype(BF16)
    acc_ref[...] += _dot(hidden, w2_ref[...])

    @pl.when(f == pl.num_programs(1) - 1)
    def _():
        z = _layer_norm(DEEPNORM_ALPHA * x_ref[...] + acc_ref[...], g_ref[...], b_ref[...])
        xo_ref[...] = z
        xbo_ref[...] = z.astype(BF16)


def _ffn(xb, x, w1, w3, w2, g, b):
    t = x.shape[0]
    d_ff = w1.shape[1]
    tm, tf = TM_FFN, TF_FFN
    row = lambda: pl.BlockSpec((tm, D_MODEL), lambda i, f: (i, 0))
    est = (2 * tm * D_MODEL * 12 + tm * D_MODEL * 4 + 2 * 3 * D_MODEL * tf * 2
           + 4 * tm * tf * 4 + (8 << 20))
    return pl.pallas_call(
        _ffn_kernel,
        grid=(t // tm, d_ff // tf),
        in_specs=[row(), row(),
                  pl.BlockSpec((D_MODEL, tf), lambda i, f: (0, f)),
                  pl.BlockSpec((D_MODEL, tf), lambda i, f: (0, f)),
                  pl.BlockSpec((tf, D_MODEL), lambda i, f: (f, 0)),
                  _resident((1, D_MODEL)), _resident((1, D_MODEL))],
        out_specs=(row(), row()),
        out_shape=(jax.ShapeDtypeStruct((t, D_MODEL), F32), jax.ShapeDtypeStruct((t, D_MODEL), BF16)),
        scratch_shapes=[pltpu.VMEM((tm, D_MODEL), F32)],
        compiler_params=_cparams(("parallel", "arbitrary"), est),
        name="ffn_ln",
    )(xb, x, w1, w3, w2, g, b)


def _router_kernel(x_ref, wr_ref, idx_ref, gate_ref):
    logits = jnp.dot(x_ref[...], wr_ref[...], preferred_element_type=F32,
                     precision=lax.Precision.HIGHEST)
    lane = lax.broadcasted_iota(jnp.int32, logits.shape, 1)
    logits = jnp.where(lane < N_EXPERTS, logits, -jnp.inf)
    m1 = jnp.max(logits, axis=1, keepdims=True)
    i1 = jnp.min(jnp.where(logits == m1, lane, LANES), axis=1, keepdims=True)
    rest = jnp.where(lane == i1, -jnp.inf, logits)
    m2 = jnp.max(rest, axis=1, keepdims=True)
    i2 = jnp.min(jnp.where(rest == m2, lane, LANES), axis=1, keepdims=True)
    e = jnp.exp(m2 - m1)
    inv = 1.0 / (1.0 + e)
    idx_ref[...] = jnp.where(lane == 0, i1, jnp.where(lane == 1, i2, 0))
    gate_ref[...] = jnp.where(lane == 0, inv, jnp.where(lane == 1, e * inv, 0.0))


def _router(x, router_padded):
    t = x.shape[0]
    tm = TM_ROUTE
    est = 2 * (tm * D_MODEL * 4 + 2 * tm * LANES * 4) + D_MODEL * LANES * 4 + (8 << 20)
    return pl.pallas_call(
        _router_kernel,
        grid=(t // tm,),
        in_specs=[pl.BlockSpec((tm, D_MODEL), lambda i: (i, 0)), _resident((D_MODEL, LANES))],
        out_specs=(pl.BlockSpec((tm, LANES), lambda i: (i, 0)), pl.BlockSpec((tm, LANES), lambda i: (i, 0))),
        out_shape=(jax.ShapeDtypeStruct((t, LANES), jnp.int32), jax.ShapeDtypeStruct((t, LANES), F32)),
        compiler_params=_cparams(("parallel",), est),
        name="moe_router",
    )(x, router_padded)


def _row_copy(src_hbm, row, dst, slot, sem):
    return pltpu.make_async_copy(src_hbm.at[pl.ds(row, 1)], dst.at[pl.ds(slot, 1)], sem)


def _gather_kernel(tok_ref, x_hbm, o_ref, buf_ref, sem):
    n = buf_ref.shape[0]

    def start(r, carry):
        _row_copy(x_hbm, tok_ref[r], buf_ref, r, sem).start()
        return carry

    def wait(r, carry):
        _row_copy(x_hbm, 0, buf_ref, r, sem).wait()
        return carry

    lax.fori_loop(0, n, start, 0)
    lax.fori_loop(0, n, wait, 0)
    o_ref[...] = buf_ref[...].astype(BF16)


def _gather_rows(x, row_tok):
    p = row_tok.shape[0]
    r = ROW_DMA_TILE
    est = r * D_MODEL * 4 + 2 * r * D_MODEL * 2 + (4 << 20)
    return pl.pallas_call(
        _gather_kernel,
        grid=(p // r,),
        in_specs=[pl.BlockSpec((r,), lambda i: (i,), memory_space=pltpu.SMEM),
                  pl.BlockSpec(memory_space=pl.ANY)],
        out_specs=pl.BlockSpec((r, D_MODEL), lambda i: (i, 0)),
        out_shape=jax.ShapeDtypeStruct((p, D_MODEL), BF16),
        scratch_shapes=[pltpu.VMEM((r, D_MODEL), F32), pltpu.SemaphoreType.DMA(())],
        compiler_params=_cparams(("arbitrary",), est),
        name="moe_gather",
    )(row_tok, x)


def _moe_ffn_kernel(te_ref, nv_ref, xs_ref, gate_ref, w1_ref, w3_ref, w2_ref, y_ref):
    i = pl.program_id(0)
    f = pl.program_id(1)
    last = pl.num_programs(1) - 1
    nv = nv_ref[i]
    for s in range(MOE_TILE // MOE_SUB):
        rows = slice(s * MOE_SUB, (s + 1) * MOE_SUB)
        live = s * MOE_SUB < nv

        @pl.when(live)
        def _():
            xb = xs_ref[rows, :]
            hidden = (_silu(_dot(xb, w1_ref[...])) * _dot(xb, w3_ref[...])).astype(BF16)
            part = _dot(hidden, w2_ref[...])

            @pl.when(f == 0)
            def _():
                y_ref[rows, :] = part

            @pl.when(f > 0)
            def _():
                y_ref[rows, :] += part

            @pl.when(f == last)
            def _():
                y_ref[rows, :] = y_ref[rows, :] * gate_ref[rows, :]

        @pl.when(jnp.logical_and(jnp.logical_not(live), f == 0))
        def _():
            y_ref[rows, :] = jnp.zeros((MOE_SUB, D_MODEL), F32)


def _moe_ffn(tile_e, tile_nv, xs, row_gate, w1, w3, w2):
    p = xs.shape[0]
    d_ff = w1.shape[2]
    tf = TF_FFN
    nf = d_ff // tf
    nt = p // MOE_TILE

    def fsel(i, f, te, nv):
        return jnp.where(nv[i] > 0, f, nf - 1)

    est = (2 * MOE_TILE * D_MODEL * 2 + 2 * MOE_TILE * D_MODEL * 4 + 2 * MOE_TILE * LANES * 4
           + 2 * 3 * D_MODEL * tf * 2 + 4 * MOE_SUB * tf * 4 + (8 << 20))
    grid_spec = pltpu.PrefetchScalarGridSpec(
        num_scalar_prefetch=2,
        grid=(nt, nf),
        in_specs=[
            pl.BlockSpec((MOE_TILE, D_MODEL), lambda i, f, te, nv: (i, 0)),
            pl.BlockSpec((MOE_TILE, 1), lambda i, f, te, nv: (i, 0)),
            pl.BlockSpec((None, D_MODEL, tf), lambda i, f, te, nv: (te[i], 0, fsel(i, f, te, nv))),
            pl.BlockSpec((None, D_MODEL, tf), lambda i, f, te, nv: (te[i], 0, fsel(i, f, te, nv))),
            pl.BlockSpec((None, tf, D_MODEL), lambda i, f, te, nv: (te[i], fsel(i, f, te, nv), 0)),
        ],
        out_specs=pl.BlockSpec((MOE_TILE, D_MODEL), lambda i, f, te, nv: (i, 0)),
    )
    return pl.pallas_call(
        _moe_ffn_kernel,
        grid_spec=grid_spec,
        out_shape=jax.ShapeDtypeStruct((p, D_MODEL), F32),
        compiler_params=_cparams(("parallel", "arbitrary"), est),
        name="moe_ffn",
    )(tile_e, tile_nv, xs, row_gate, w1, w3, w2)


def _combine_kernel(p0_ref, p1_ref, ys_hbm, x_ref, g_ref, b_ref, xo_ref, xbo_ref, buf_ref, sem):
    n = x_ref.shape[0]

    def start(r, carry):
        _row_copy(ys_hbm, p0_ref[r], buf_ref.at[0], r, sem).start()
        _row_copy(ys_hbm, p1_ref[r], buf_ref.at[1], r, sem).start()
        return carry

    def wait(r, carry):
        _row_copy(ys_hbm, 0, buf_ref.at[0], r, sem).wait()
        _row_copy(ys_hbm, 0, buf_ref.at[1], r, sem).wait()
        return carry

    lax.fori_loop(0, n, start, 0)
    lax.fori_loop(0, n, wait, 0)
    y = buf_ref[0] + buf_ref[1]
    z = _layer_norm(DEEPNORM_ALPHA * x_ref[...] + y, g_ref[...], b_ref[...])
    xo_ref[...] = z
    xbo_ref[...] = z.astype(BF16)


def _moe_combine(pos0, pos1, ys, x, g, b):
    t = x.shape[0]
    r = ROW_DMA_TILE
    row = lambda: pl.BlockSpec((r, D_MODEL), lambda i: (i, 0))
    smem = lambda: pl.BlockSpec((r,), lambda i: (i,), memory_space=pltpu.SMEM)
    est = 2 * r * D_MODEL * 4 + 2 * r * D_MODEL * 10 + (8 << 20)
    return pl.pallas_call(
        _combine_kernel,
        grid=(t // r,),
        in_specs=[smem(), smem(), pl.BlockSpec(memory_space=pl.ANY), row(),
                  _resident((1, D_MODEL)), _resident((1, D_MODEL))],
        out_specs=(row(), row()),
        out_shape=(jax.ShapeDtypeStruct((t, D_MODEL), F32), jax.ShapeDtypeStruct((t, D_MODEL), BF16)),
        scratch_shapes=[pltpu.VMEM((2, r, D_MODEL), F32), pltpu.SemaphoreType.DMA(())],
        compiler_params=_cparams(("arbitrary",), est),
        name="moe_combine_ln",
    )(pos0, pos1, ys, x, g, b)


def _moe_plan(idx, gates, n_tiles):
    t = idx.shape[0]
    a = t * TOP_K
    e_flat = idx.reshape(a)
    onehot = (e_flat[:, None] == jnp.arange(N_EXPERTS, dtype=jnp.int32)[None, :]).astype(jnp.int32)
    csum = jnp.cumsum(onehot, axis=0)
    rank = jnp.sum(onehot * (csum - 1), axis=1)
    counts = csum[-1]
    tiles_e = (counts + MOE_TILE - 1) // MOE_TILE
    tile_end = jnp.cumsum(tiles_e)
    tile_start = tile_end - tiles_e
    dest = tile_start[e_flat] * MOE_TILE + rank
    p = n_tiles * MOE_TILE
    tok_flat = jnp.repeat(jnp.arange(t, dtype=jnp.int32), TOP_K)
    row_tok = jnp.zeros((p,), jnp.int32).at[dest].set(tok_flat)
    row_gate = jnp.zeros((p,), F32).at[dest].set(gates.reshape(a))
    tile_id = jnp.arange(n_tiles, dtype=jnp.int32)
    total = tile_end[-1]
    tid = jnp.minimum(tile_id, total - 1)
    tile_e = jnp.clip(jnp.searchsorted(tile_end, tid, side='right'), 0, N_EXPERTS - 1).astype(jnp.int32)
    nv = jnp.clip(counts[tile_e] - (tid - tile_start[tile_e]) * MOE_TILE, 0, MOE_TILE)
    tile_nv = jnp.where(tile_id < total, nv, 0).astype(jnp.int32)
    pos = dest.reshape(t, TOP_K).astype(jnp.int32)
    return row_tok, row_gate.reshape(p, 1), tile_e, tile_nv, pos[:, 0], pos[:, 1]


def _moe(x, router_padded, w1, w3, w2, g, b):
    t = x.shape[0]
    n_tiles = (t * TOP_K) // MOE_TILE + N_EXPERTS
    idx_l, gate_l = _router(x, router_padded)
    idx, gates = idx_l[:, :TOP_K], gate_l[:, :TOP_K]
    row_tok, row_gate, tile_e, tile_nv, pos0, pos1 = _moe_plan(idx, gates, n_tiles)
    xs = _gather_rows(x, row_tok)
    ys = _moe_ffn(tile_e, tile_nv, xs, row_gate, w1, w3, w2)
    return _moe_combine(pos0, pos1, ys, x, g, b)


def _q_head_perm():
    n = np.arange(ATTN_WIDTH)
    c, half, d = n // LANES, (n % LANES) // ATTN_HEAD_DIM, n % ATTN_HEAD_DIM
    return (c + (N_Q_HEADS // N_KV_HEADS) * half) * ATTN_HEAD_DIM + d


def _ret_dim_perm():
    m = np.arange(RET_QK_DIM)
    inner = np.where(m < RET_QK_DIM // 2, 2 * m, 2 * (m - RET_QK_DIM // 2) + 1)
    return (np.arange(N_RET_HEADS)[:, None] * RET_QK_DIM + inner[None, :]).reshape(-1)


def _in_col_perm():
    perm = np.arange(IN_WIDTH)
    perm[_C_Q:_C_K] = _C_Q + _q_head_perm()
    perm[_C_RQ:_C_RK] = _C_RQ + _ret_dim_perm()
    perm[_C_RK:_C_RV] = _C_RK + _ret_dim_perm()
    return perm


def _rotary_tables(positions):
    pos = positions.astype(F32).reshape(-1)[:, None]
    half = ATTN_HEAD_DIM // 2
    inv_a = ROPE_THETA ** (-jnp.arange(0, half, dtype=F32) * 2.0 / ATTN_HEAD_DIM)
    ang = pos * inv_a
    cos, sin = jnp.cos(ang), jnp.sin(ang)
    ca = jnp.concatenate([cos, cos, cos, cos], axis=1)
    sa = jnp.concatenate([-sin, sin, -sin, sin], axis=1)
    inv_r = 1.0 / (RET_THETA ** jnp.linspace(0.0, 1.0, RET_QK_DIM // 2, dtype=F32))
    ang = pos * inv_r
    cos, sin = jnp.cos(ang), jnp.sin(ang)
    cr = jnp.concatenate([cos, cos], axis=1)
    sr = jnp.concatenate([-sin, sin], axis=1)
    return ca, sa, cr, sr


def _retention_tables():
    h = jnp.arange(N_RET_HEADS, dtype=F32)
    lg = jnp.log(1.0 - 2.0 ** (-5.0 - h))
    c = jnp.arange(RET_CHUNK, dtype=F32)
    diff = c[:, None] - c[None, :]
    decay = jnp.where(diff >= 0, jnp.exp(lg[:, None, None] * jnp.maximum(diff, 0.0)), 0.0)
    zeta = jnp.exp(lg[:, None] * (RET_CHUNK - 1.0 - c)[None])[:, :, None]
    xi = jnp.exp(lg[:, None] * (c + 1.0)[None])[:, :, None]
    g_chunk = jnp.exp(lg * RET_CHUNK)
    return decay, zeta, xi, g_chunk


def kernel(x, mem, positions, ln_g, ln_b, w_in, sinks, w_o, xq, xk, xv, xo,
           ffn_w1, ffn_w3, ffn_w2, router, moe_w1, moe_w3, moe_w2):
    batch, seq, _ = x.shape
    mem_len = mem.shape[1]
    t = batch * seq
    ca, sa, cr, sr = _rotary_tables(positions)
    decay, zeta, xi, g_chunk = _retention_tables()
    in_perm = _in_col_perm()
    out_perm = np.concatenate([_q_head_perm(), np.arange(ATTN_WIDTH, ATTN_WIDTH + RET_V_WIDTH)])

    xf = x.reshape(t, D_MODEL)
    xb = xf.astype(BF16)
    mem_b = mem.reshape(batch * mem_len, D_MODEL).astype(BF16)
    for l in range(DEPTH):
        gl = ln_g[l].reshape(3, 1, D_MODEL)
        bl = ln_b[l].reshape(3, 1, D_MODEL)
        w_in_l = w_in[l][:, in_perm].astype(BF16)
        w_o_l = w_o[l][out_perm, :].astype(BF16)
        q, k, v, rq, rk, rv, rg = _in_proj(xb, w_in_l, ca, sa, cr, sr)
        attn = _swa_attention(q, k, v, sinks[l], batch, seq)
        ret = _retention(rq, rk, rv, rg, decay, zeta, xi, g_chunk, batch, seq)
        xf, xb = _out_proj(attn, ret, xf, w_o_l, gl[0], bl[0])
        kmem = _matmul_bf16(mem_b, xk[l].astype(BF16))
        vmem = _matmul_bf16(mem_b, xv[l].astype(BF16))
        xf, xb = _xattn(xb, xf, kmem, vmem, xq[l].astype(BF16), xo[l].astype(BF16),
                        gl[1], bl[1], seq, mem_len)
        if l % 2 == 0:
            j = l // 2
            xf, xb = _ffn(xb, xf, ffn_w1[j].astype(BF16), ffn_w3[j].astype(BF16),
                          ffn_w2[j].astype(BF16), gl[2], bl[2])
        else:
            j = l // 2
            router_padded = jnp.pad(router[j], ((0, 0), (0, LANES - N_EXPERTS)))
            xf, xb = _moe(xf, router_padded, moe_w1[j].astype(BF16), moe_w3[j].astype(BF16),
                          moe_w2[j].astype(BF16), gl[2], bl[2])
    return xf.reshape(batch, seq, D_MODEL)
```

```python
import functools

import numpy as np
import jax
import jax.numpy as jnp
from jax import lax
from jax.experimental import pallas as pl
from jax.experimental.pallas import tpu as pltpu

F32 = jnp.float32
BF16 = jnp.bfloat16

D_MODEL = 2048
DEPTH = 4
ATTN_HEAD_DIM = 64
N_Q_HEADS = 16
N_KV_HEADS = 2
WINDOW = 128
ROPE_THETA = 10000.0
ATTN_WIDTH = N_Q_HEADS * ATTN_HEAD_DIM
KV_WIDTH = N_KV_HEADS * ATTN_HEAD_DIM
N_RET_HEADS = 4
RET_QK_DIM = 128
RET_V_DIM = 256
RET_CHUNK = 128
RET_THETA = 10000.0
RET_QK_WIDTH = N_RET_HEADS * RET_QK_DIM
RET_V_WIDTH = N_RET_HEADS * RET_V_DIM
IN_WIDTH = ATTN_WIDTH + 2 * KV_WIDTH + 2 * RET_QK_WIDTH + 2 * RET_V_WIDTH
N_XATTN_HEADS = 4
XATTN_HEAD_DIM = D_MODEL // N_XATTN_HEADS
N_EXPERTS = 8
TOP_K = 2
LN_EPS = 1e-5
GN_EPS = 1e-6
DEEPNORM_ALPHA = (2 * DEPTH) ** 0.25

_C_Q = 0
_C_K = _C_Q + ATTN_WIDTH
_C_V = _C_K + KV_WIDTH
_C_RQ = _C_V + KV_WIDTH
_C_RK = _C_RQ + RET_QK_WIDTH
_C_RV = _C_RK + RET_QK_WIDTH
_C_RG = _C_RV + RET_V_WIDTH

LANES = 128
VMEM_PHYSICAL_BYTES = 64 << 20
VMEM_RESERVE_BYTES = 6 << 20

TM_PROJ = 512
TQ_ATTN = 512
RET_ROWS = 512
TM_OUT = 512
TM_XATTN = 256
TM_FFN = 512
TF_FFN = 512
TF_MOE = 256
TM_ROUTE = 512
MOE_TILE = 1024
MOE_SUB = 256
ROW_DMA_TILE = 256
ROW_DMA_UNROLL = 8
NEG_BIG = -1e30


def _cparams(semantics, vmem_estimate_bytes):
    limit = min(int(vmem_estimate_bytes) + VMEM_RESERVE_BYTES, VMEM_PHYSICAL_BYTES - (4 << 20))
    return pltpu.CompilerParams(dimension_semantics=semantics, vmem_limit_bytes=limit)


def _resident(shape):
    return pl.BlockSpec(shape, lambda *_: (0,) * len(shape), pipeline_mode=pl.Buffered(1))


def _layer_norm(y, g, b):
    mu = jnp.mean(y, axis=-1, keepdims=True)
    d = y - mu
    var = jnp.mean(d * d, axis=-1, keepdims=True)
    return d * lax.rsqrt(var + LN_EPS) * g + b


def _silu(x):
    return x / (1.0 + jnp.exp(-x))


def _dot(a, b):
    return jnp.dot(a, b, preferred_element_type=F32)


def _dot_nt(a, b):
    return lax.dot_general(a, b, (((1,), (1,)), ((), ())), preferred_element_type=F32)


def _dot_tn(a, b):
    return lax.dot_general(a, b, (((0,), (0,)), ((), ())), preferred_element_type=F32)


def _in_proj_kernel(xb_ref, w_ref, ca_ref, sa_ref, cr_ref, sr_ref,
                    q_ref, k_ref, v_ref, rq_ref, rk_ref, rv_ref, rg_ref):
    xb = xb_ref[...]
    tm = xb.shape[0]
    lane = lax.broadcasted_iota(jnp.int32, (tm, LANES), 1)
    low_half = (lane % ATTN_HEAD_DIM) < (ATTN_HEAD_DIM // 2)
    ca, sa = ca_ref[...], sa_ref[...]
    cr, sr = cr_ref[...], sr_ref[...]

    def rope_pair(x):
        swapped = jnp.where(low_half, pltpu.roll(x, LANES - 32, 1), pltpu.roll(x, 32, 1))
        return x * ca + swapped * sa

    def ret_rotate(x):
        return x * cr + pltpu.roll(x, LANES // 2, 1) * sr

    yq = _dot(xb, w_ref[:, _C_Q:_C_K])
    for c in range(ATTN_WIDTH // LANES):
        y = rope_pair(yq[:, c * LANES:(c + 1) * LANES]) * (ATTN_HEAD_DIM ** -0.5)
        q_ref[:, c * LANES:(c + 1) * LANES] = y.astype(BF16)
    ykv = _dot(xb, w_ref[:, _C_K:_C_RQ])
    k_ref[...] = rope_pair(ykv[:, :KV_WIDTH]).astype(BF16)
    v_ref[...] = ykv[:, KV_WIDTH:].astype(BF16)
    yr = _dot(xb, w_ref[:, _C_RQ:_C_RV])
    for h in range(N_RET_HEADS):
        lo = h * RET_QK_DIM
        y = ret_rotate(yr[:, lo:lo + RET_QK_DIM]) * (RET_QK_DIM ** -0.5)
        rq_ref[:, lo:lo + RET_QK_DIM] = y.astype(BF16)
        lo += RET_QK_WIDTH
        rk_ref[:, h * RET_QK_DIM:(h + 1) * RET_QK_DIM] = ret_rotate(yr[:, lo:lo + RET_QK_DIM])
    rv_ref[...] = _dot(xb, w_ref[:, _C_RV:_C_RG]).astype(BF16)
    rg_ref[...] = _dot(xb, w_ref[:, _C_RG:IN_WIDTH])


def _in_proj(xb, w_in, ca, sa, cr, sr):
    t = xb.shape[0]
    tm = TM_PROJ
    row = lambda w: pl.BlockSpec((tm, w), lambda i: (i, 0))
    out_shapes = (
        jax.ShapeDtypeStruct((t, ATTN_WIDTH), BF16),
        jax.ShapeDtypeStruct((t, KV_WIDTH), BF16),
        jax.ShapeDtypeStruct((t, KV_WIDTH), BF16),
        jax.ShapeDtypeStruct((t, RET_QK_WIDTH), BF16),
        jax.ShapeDtypeStruct((t, RET_QK_WIDTH), F32),
        jax.ShapeDtypeStruct((t, RET_V_WIDTH), BF16),
        jax.ShapeDtypeStruct((t, RET_V_WIDTH), F32),
    )
    out_bytes = tm * (2 * ATTN_WIDTH + 4 * KV_WIDTH + 6 * RET_QK_WIDTH + 6 * RET_V_WIDTH)
    est = D_MODEL * IN_WIDTH * 2 + 2 * (tm * D_MODEL * 2 + 4 * tm * LANES * 4 + out_bytes) + (8 << 20)
    return pl.pallas_call(
        _in_proj_kernel,
        grid=(t // tm,),
        in_specs=[row(D_MODEL), _resident((D_MODEL, IN_WIDTH)),
                  row(LANES), row(LANES), row(LANES), row(LANES)],
        out_specs=(row(ATTN_WIDTH), row(KV_WIDTH), row(KV_WIDTH), row(RET_QK_WIDTH),
                   row(RET_QK_WIDTH), row(RET_V_WIDTH), row(RET_V_WIDTH)),
        out_shape=out_shapes,
        compiler_params=_cparams(("parallel",), est),
        name="in_proj",
    )(xb, w_in, ca, sa, cr, sr)


def _swa_kernel(sinks_ref, q_ref, kc_ref, vc_ref, kp_ref, vp_ref, o_ref, kb_ref, vb_ref):
    i = pl.program_id(1)
    w = WINDOW
    kb_ref[0:w, :] = kp_ref[...]
    kb_ref[w:, :] = kc_ref[...]
    vb_ref[0:w, :] = vp_ref[...]
    vb_ref[w:, :] = vc_ref[...]

    r2 = lax.broadcasted_iota(jnp.int32, (2 * w, 2 * w), 0) % w
    c2 = lax.broadcasted_iota(jnp.int32, (2 * w, 2 * w), 1)
    band = (c2 > r2) & (c2 <= r2 + w)
    row_q = lax.broadcasted_iota(jnp.int32, (2 * w, LANES), 0)
    lane_q = lax.broadcasted_iota(jnp.int32, (2 * w, LANES), 1)
    keep_q = (row_q < w) == (lane_q < ATTN_HEAD_DIM)
    row_s = lax.broadcasted_iota(jnp.int32, (2 * w, 1), 0)
    lane_o = lax.broadcasted_iota(jnp.int32, (w, LANES), 1)
    n_pairs = ATTN_WIDTH // LANES

    def sub_block(j, carry):
        r0 = pl.multiple_of(j * w, w)
        kband = kb_ref[pl.ds(r0, 2 * w), :]
        vband = vb_ref[pl.ds(r0, 2 * w), :]
        has_prev = jnp.logical_or(j > 0, i > 0)
        mask = band & jnp.logical_or(c2 >= w, has_prev)
        for c in range(n_pairs):
            qc = q_ref[pl.ds(r0, w), c * LANES:(c + 1) * LANES]
            q2 = jnp.concatenate([qc, qc], axis=0)
            q2 = jnp.where(keep_q, q2, jnp.zeros_like(q2))
            s = _dot_nt(q2, kband)
            s = jnp.where(mask, s, NEG_BIG)
            sink = jnp.where(row_s < w, sinks_ref[c], sinks_ref[n_pairs + c])
            m = jnp.maximum(jnp.max(s, axis=1, keepdims=True), sink)
            p = jnp.exp(s - m)
            denom = jnp.sum(p, axis=1, keepdims=True) + jnp.exp(sink - m)
            p = p * (1.0 / denom)
            pv = _dot(p.astype(BF16), vband)
            o = jnp.where(lane_o < ATTN_HEAD_DIM, pv[:w], pv[w:])
            o_ref[pl.ds(r0, w), c * LANES:(c + 1) * LANES] = o.astype(BF16)
        return carry

    lax.fori_loop(0, q_ref.shape[0] // w, sub_block, 0)


def _swa_attention(q, k, v, sinks, batch, seq):
    t = q.shape[0]
    tq = TQ_ATTN
    nq = seq // tq
    blocks_per_step = tq // WINDOW
    blocks_per_seq = seq // WINDOW
    cur = lambda wd: pl.BlockSpec((tq, wd), lambda b, i: (b * nq + i, 0))
    prev = pl.BlockSpec(
        (WINDOW, KV_WIDTH),
        lambda b, i: (jnp.maximum(b * blocks_per_seq + i * blocks_per_step - 1, 0), 0))
    est = 2 * (2 * tq * ATTN_WIDTH * 2 + 2 * tq * KV_WIDTH * 2) + (8 << 20)
    return pl.pallas_call(
        _swa_kernel,
        grid=(batch, nq),
        in_specs=[pl.BlockSpec(memory_space=pltpu.SMEM),
                  cur(ATTN_WIDTH), cur(KV_WIDTH), cur(KV_WIDTH), prev, prev],
        out_specs=cur(ATTN_WIDTH),
        out_shape=jax.ShapeDtypeStruct((t, ATTN_WIDTH), BF16),
        scratch_shapes=[pltpu.VMEM((tq + WINDOW, KV_WIDTH), BF16),
                        pltpu.VMEM((tq + WINDOW, KV_WIDTH), BF16)],
        compiler_params=_cparams(("parallel", "parallel"), est),
        name="swa_attention",
    )(sinks, q, k, v, k, v)


def _retention_kernel(gch_ref, rq_ref, rk_ref, rv_ref, rg_ref, decay_ref, zeta_ref, xi_ref,
                      o_ref, state_ref):
    h = pl.program_id(1)
    n = pl.program_id(2)
    c = RET_CHUNK

    @pl.when(n == 0)
    def _():
        state_ref[...] = jnp.zeros_like(state_ref)

    decay = decay_ref[...]
    zeta = zeta_ref[...]
    xi = xi_ref[...]
    g_chunk = gch_ref[h]
    for j in range(rq_ref.shape[0] // c):
        rows = slice(j * c, (j + 1) * c)
        q = rq_ref[rows, :]
        kf = rk_ref[rows, :]
        v = rv_ref[rows, :]
        scores = _dot_nt(q, kf.astype(BF16)) * decay
        intra = _dot(scores.astype(BF16), v)
        state = state_ref[...]
        inter = _dot(q, state.astype(BF16)) * xi
        kv = _dot_tn((kf * zeta).astype(BF16), v)
        state_ref[...] = g_chunk * state + kv
        o = intra + inter
        mu = jnp.mean(o, axis=-1, keepdims=True)
        d = o - mu
        var = jnp.mean(d * d, axis=-1, keepdims=True)
        normed = d * lax.rsqrt(var + GN_EPS)
        o_ref[rows, :] = (_silu(rg_ref[rows, :]) * normed).astype(BF16)


def _retention(rq, rk, rv, rg, decay, zeta, xi, g_chunk, batch, seq):
    t = rq.shape[0]
    r = RET_ROWS
    ns = seq // r
    qk = lambda: pl.BlockSpec((r, RET_QK_DIM), lambda b, h, n: (b * ns + n, h))
    vv = lambda: pl.BlockSpec((r, RET_V_DIM), lambda b, h, n: (b * ns + n, h))
    per_head = lambda wd: pl.BlockSpec((None, RET_CHUNK, wd), lambda b, h, n: (h, 0, 0))
    est = 2 * r * (RET_QK_DIM * 6 + RET_V_DIM * 8) + (8 << 20)
    return pl.pallas_call(
        _retention_kernel,
        grid=(batch, N_RET_HEADS, ns),
        in_specs=[pl.BlockSpec(memory_space=pltpu.SMEM),
                  qk(), qk(), vv(), vv(), per_head(RET_CHUNK), per_head(1), per_head(1)],
        out_specs=vv(),
        out_shape=jax.ShapeDtypeStruct((t, RET_V_WIDTH), BF16),
        scratch_shapes=[pltpu.VMEM((RET_QK_DIM, RET_V_DIM), F32)],
        compiler_params=_cparams(("parallel", "parallel", "arbitrary"), est),
        name="retention",
    )(g_chunk, rq, rk, rv, rg, decay, zeta, xi)


def _out_proj_kernel(a_ref, r_ref, x_ref, w_ref, g_ref, b_ref, xo_ref, xbo_ref):
    y = _dot(a_ref[...], w_ref[0:ATTN_WIDTH, :]) + _dot(r_ref[...], w_ref[ATTN_WIDTH:, :])
    z = _layer_norm(DEEPNORM_ALPHA * x_ref[...] + y, g_ref[...], b_ref[...])
    xo_ref[...] = z
    xbo_ref[...] = z.astype(BF16)


def _out_proj(a, r, x, w_o, g, b):
    t = x.shape[0]
    tm = TM_OUT
    row = lambda wd: pl.BlockSpec((tm, wd), lambda i: (i, 0))
    est = D_MODEL * D_MODEL * 2 + 2 * tm * (2 * 2 * ATTN_WIDTH + 10 * D_MODEL) + (8 << 20)
    return pl.pallas_call(
        _out_proj_kernel,
        grid=(t // tm,),
        in_specs=[row(ATTN_WIDTH), row(RET_V_WIDTH), row(D_MODEL), _resident((D_MODEL, D_MODEL)),
                  _resident((1, D_MODEL)), _resident((1, D_MODEL))],
        out_specs=(row(D_MODEL), row(D_MODEL)),
        out_shape=(jax.ShapeDtypeStruct((t, D_MODEL), F32), jax.ShapeDtypeStruct((t, D_MODEL), BF16)),
        compiler_params=_cparams(("parallel",), est),
        name="out_proj_ln",
    )(a, r, x, w_o, g, b)


def _matmul_kernel(a_ref, w_ref, o_ref):
    o_ref[...] = _dot(a_ref[...], w_ref[...]).astype(o_ref.dtype)


def _matmul_bf16(a, w, tn=512):
    m, kdim = a.shape
    n = w.shape[1]
    est = 2 * (m * kdim * 2 + kdim * tn * 2 + m * tn * 2) + (4 << 20)
    return pl.pallas_call(
        _matmul_kernel,
        grid=(n // tn,),
        in_specs=[pl.BlockSpec((m, kdim), lambda j: (0, 0)), pl.BlockSpec((kdim, tn), lambda j: (0, j))],
        out_specs=pl.BlockSpec((m, tn), lambda j: (0, j)),
        out_shape=jax.ShapeDtypeStruct((m, n), BF16),
        compiler_params=_cparams(("parallel",), est),
        name="mem_kv_proj",
    )(a, w)


def _pack_bf16_pairs(z):
    half = z.shape[1] // 2
    lo = lax.bitcast_convert_type(z[:, :half].astype(BF16).astype(F32), jnp.uint32)
    hi = lax.bitcast_convert_type(z[:, half:].astype(BF16).astype(F32), jnp.uint32)
    return (lo >> 16) | hi


def _unpack_bf16_pairs(w):
    lo = lax.bitcast_convert_type(w << 16, F32).astype(BF16)
    hi = lax.bitcast_convert_type(w & jnp.uint32(0xFFFF0000), F32).astype(BF16)
    return lo, hi


def _xattn_kernel(xb_ref, x_ref, k_ref, v_ref, wq_ref, wo_ref, g_ref, b_ref, xo_ref, xbo_ref, *, packed):
    q = _dot(xb_ref[...], wq_ref[...]).astype(BF16)
    scale = XATTN_HEAD_DIM ** -0.5
    heads = []
    for h in range(N_XATTN_HEADS):
        cols = slice(h * XATTN_HEAD_DIM, (h + 1) * XATTN_HEAD_DIM)
        s = _dot_nt(q[:, cols], k_ref[:, cols]) * scale
        m = jnp.max(s, axis=1, keepdims=True)
        p = jnp.exp(s - m)
        p = p * (1.0 / jnp.sum(p, axis=1, keepdims=True))
        heads.append(_dot(p.astype(BF16), v_ref[:, cols]).astype(BF16))
    o = jnp.concatenate(heads, axis=1)
    y = _dot(o, wo_ref[...])
    z = _layer_norm(DEEPNORM_ALPHA * x_ref[...] + y, g_ref[...], b_ref[...])
    xo_ref[...] = z
    xbo_ref[...] = _pack_bf16_pairs(z) if packed else z.astype(BF16)


def _xattn(xb, x, kmem, vmem, wq, wo, g, b, seq, mem_len, packed):
    t = x.shape[0]
    tm = TM_XATTN
    row = lambda: pl.BlockSpec((tm, D_MODEL), lambda i: (i, 0))
    mem = lambda: pl.BlockSpec((mem_len, D_MODEL), lambda i: ((i * tm) // seq, 0))
    if packed:
        second_spec = pl.BlockSpec((tm, D_MODEL // 2), lambda i: (i, 0))
        second_shape = jax.ShapeDtypeStruct((t, D_MODEL // 2), jnp.uint32)
    else:
        second_spec = row()
        second_shape = jax.ShapeDtypeStruct((t, D_MODEL), BF16)
    est = (2 * D_MODEL * D_MODEL * 2 + 2 * 2 * mem_len * D_MODEL * 2
           + 2 * tm * D_MODEL * 12 + 6 * tm * D_MODEL * 4 + (8 << 20))
    return pl.pallas_call(
        functools.partial(_xattn_kernel, packed=packed),
        grid=(t // tm,),
        in_specs=[row(), row(), mem(), mem(), _resident((D_MODEL, D_MODEL)), _resident((D_MODEL, D_MODEL)),
                  _resident((1, D_MODEL)), _resident((1, D_MODEL))],
        out_specs=(row(), second_spec),
        out_shape=(jax.ShapeDtypeStruct((t, D_MODEL), F32), second_shape),
        compiler_params=_cparams(("parallel",), est),
        name="mem_xattn_ln",
    )(xb, x, kmem, vmem, wq, wo, g, b)


def _ffn_kernel(xb_ref, x_ref, w1_ref, w3_ref, w2_ref, g_ref, b_ref, xo_ref, xbo_ref, acc_ref):
    f = pl.program_id(1)

    @pl.when(f == 0)
    def _():
        acc_ref[...] = jnp.zeros_like(acc_ref)

    xb = xb_ref[...]
    hidden = (_silu(_dot(xb, w1_ref[...])) * _dot(xb, w3_ref[...])).astype(BF16)
    acc_ref[...] += _dot(hidden, w2_ref[...])

    @pl.when(f == pl.num_programs(1) - 1)
    def _():
        z = _layer_norm(DEEPNORM_ALPHA * x_ref[...] + acc_ref[...], g_ref[...], b_ref[...])
        xo_ref[...] = z
        xbo_ref[...] = z.astype(BF16)


def _ffn(xb, x, w1, w3, w2, g, b):
    t = x.shape[0]
    d_ff = w1.shape[1]
    tm, tf = TM_FFN, TF_FFN
    row = lambda: pl.BlockSpec((tm, D_MODEL), lambda i, f: (i, 0))
    est = (2 * tm * D_MODEL * 12 + tm * D_MODEL * 4 + 2 * 3 * D_MODEL * tf * 2
           + 4 * tm * tf * 4 + (8 << 20))
    return pl.pallas_call(
        _ffn_kernel,
        grid=(t // tm, d_ff // tf),
        in_specs=[row(), row(),
                  pl.BlockSpec((D_MODEL, tf), lambda i, f: (0, f)),
                  pl.BlockSpec((D_MODEL, tf), lambda i, f: (0, f)),
                  pl.BlockSpec((tf, D_MODEL), lambda i, f: (f, 0)),
                  _resident((1, D_MODEL)), _resident((1, D_MODEL))],
        out_specs=(row(), row()),
        out_shape=(jax.ShapeDtypeStruct((t, D_MODEL), F32), jax.ShapeDtypeStruct((t, D_MODEL), BF16)),
        scratch_shapes=[pltpu.VMEM((tm, D_MODEL), F32)],
        compiler_params=_cparams(("parallel", "arbitrary"), est),
        name="ffn_ln",
    )(xb, x, w1, w3, w2, g, b)


def _router_kernel(x_ref, wr_ref, idx_ref, gate_ref):
    logits = jnp.dot(x_ref[...], wr_ref[...], preferred_element_type=F32,
                     precision=lax.Precision.HIGHEST)
    lane = lax.broadcasted_iota(jnp.int32, logits.shape, 1)
    logits = jnp.where(lane < N_EXPERTS, logits, -jnp.inf)
    m1 = jnp.max(logits, axis=1, keepdims=True)
    i1 = jnp.min(jnp.where(logits == m1, lane, LANES), axis=1, keepdims=True)
    rest = jnp.where(lane == i1, -jnp.inf, logits)
    m2 = jnp.max(rest, axis=1, keepdims=True)
    i2 = jnp.min(jnp.where(rest == m2, lane, LANES), axis=1, keepdims=True)
    e = jnp.exp(m2 - m1)
    inv = 1.0 / (1.0 + e)
    idx_ref[...] = jnp.where(lane == 0, i1, jnp.where(lane == 1, i2, 0))
    gate_ref[...] = jnp.where(lane == 0, inv, jnp.where(lane == 1, e * inv, 0.0))


def _router(x, router_padded):
    t = x.shape[0]
    tm = TM_ROUTE
    est = 2 * (tm * D_MODEL * 4 + 2 * tm * LANES * 4) + D_MODEL * LANES * 4 + (8 << 20)
    return pl.pallas_call(
        _router_kernel,
        grid=(t // tm,),
        in_specs=[pl.BlockSpec((tm, D_MODEL), lambda i: (i, 0)), _resident((D_MODEL, LANES))],
        out_specs=(pl.BlockSpec((tm, LANES), lambda i: (i, 0)), pl.BlockSpec((tm, LANES), lambda i: (i, 0))),
        out_shape=(jax.ShapeDtypeStruct((t, LANES), jnp.int32), jax.ShapeDtypeStruct((t, LANES), F32)),
        compiler_params=_cparams(("parallel",), est),
        name="moe_router",
    )(x, router_padded)


def _scatter_kernel(d0_ref, d1_ref, xp_ref, init_hbm, xs_hbm, sem):
    del init_hbm
    n = xp_ref.shape[0]

    def row_copy(r, dst_row):
        return pltpu.make_async_copy(xp_ref.at[pl.ds(r, 1)], xs_hbm.at[pl.ds(dst_row, 1)], sem)

    def start(r, carry):
        row_copy(r, d0_ref[r]).start()
        row_copy(r, d1_ref[r]).start()
        return carry

    def wait(r, carry):
        row_copy(r, 0).wait()
        row_copy(r, 0).wait()
        return carry

    lax.fori_loop(0, n, start, 0, unroll=ROW_DMA_UNROLL)
    lax.fori_loop(0, n, wait, 0, unroll=ROW_DMA_UNROLL)


def _moe_scatter(pos0, pos1, xp, n_rows):
    t, width = xp.shape
    r = ROW_DMA_TILE
    smem = lambda: pl.BlockSpec((r,), lambda i: (i,), memory_space=pltpu.SMEM)
    est = 2 * r * width * 4 + (4 << 20)
    return pl.pallas_call(
        _scatter_kernel,
        grid=(t // r,),
        in_specs=[smem(), smem(), pl.BlockSpec((r, width), lambda i: (i, 0)),
                  pl.BlockSpec(memory_space=pl.ANY)],
        out_specs=pl.BlockSpec(memory_space=pl.ANY),
        out_shape=jax.ShapeDtypeStruct((n_rows, width), xp.dtype),
        scratch_shapes=[pltpu.SemaphoreType.DMA(())],
        input_output_aliases={3: 0},
        compiler_params=_cparams(("arbitrary",), est),
        name="moe_scatter",
    )(pos0, pos1, xp, jnp.zeros((n_rows, width), xp.dtype))


def _moe_ffn_kernel(te_ref, nv_ref, xs_ref, w1_ref, w3_ref, w2_ref, y_ref, xbf_ref):
    i = pl.program_id(0)
    f = pl.program_id(1)
    nv = nv_ref[i]
    half = D_MODEL // 2

    def unpack(rows):
        lo, hi = _unpack_bf16_pairs(xs_ref[rows, :])
        xbf_ref[rows, 0:half] = lo
        xbf_ref[rows, half:] = hi

    def accumulate(rows):
        xb = xbf_ref[rows, :]
        gate_proj = _dot(xb, w1_ref[...].astype(BF16))
        up_proj = _dot(xb, w3_ref[...].astype(BF16))
        hidden = (_silu(gate_proj) * up_proj).astype(BF16)
        y_ref[rows, :] += _dot(hidden, w2_ref[...].astype(BF16))

    @pl.when(f == 0)
    def _():
        y_ref[...] = jnp.zeros_like(y_ref)

    @pl.when(nv == MOE_TILE)
    def _():
        whole = slice(0, MOE_TILE)

        @pl.when(f == 0)
        def _():
            unpack(whole)

        accumulate(whole)

    @pl.when(nv < MOE_TILE)
    def _():
        for s in range(MOE_TILE // MOE_SUB):
            rows = slice(s * MOE_SUB, (s + 1) * MOE_SUB)

            @pl.when(s * MOE_SUB < nv)
            def _():
                @pl.when(f == 0)
                def _():
                    unpack(rows)

                accumulate(rows)


def _moe_ffn(tile_e, tile_nv, xs, w1, w3, w2, layer):
    p = xs.shape[0]
    d_ff = w1.shape[3]
    tf = TF_MOE
    nf = d_ff // tf
    nt = p // MOE_TILE

    def fsel(i, f, nv):
        return jnp.where(nv[i] > 0, f, nf - 1)

    est = (2 * MOE_TILE * D_MODEL * 2 + MOE_TILE * D_MODEL * 2 + 2 * MOE_TILE * D_MODEL * 4
           + 2 * 3 * D_MODEL * tf * 4 + MOE_TILE * D_MODEL * 4 + (8 << 20))
    grid_spec = pltpu.PrefetchScalarGridSpec(
        num_scalar_prefetch=2,
        grid=(nt, nf),
        in_specs=[
            pl.BlockSpec((MOE_TILE, D_MODEL // 2), lambda i, f, te, nv: (i, 0)),
            pl.BlockSpec((None, None, D_MODEL, tf), lambda i, f, te, nv: (layer, te[i], 0, fsel(i, f, nv))),
            pl.BlockSpec((None, None, D_MODEL, tf), lambda i, f, te, nv: (layer, te[i], 0, fsel(i, f, nv))),
            pl.BlockSpec((None, None, tf, D_MODEL), lambda i, f, te, nv: (layer, te[i], fsel(i, f, nv), 0)),
        ],
        out_specs=pl.BlockSpec((MOE_TILE, D_MODEL), lambda i, f, te, nv: (i, 0)),
        scratch_shapes=[pltpu.VMEM((MOE_TILE, D_MODEL), BF16)],
    )
    return pl.pallas_call(
        _moe_ffn_kernel,
        grid_spec=grid_spec,
        out_shape=jax.ShapeDtypeStruct((p, D_MODEL), F32),
        compiler_params=_cparams(("parallel", "arbitrary"), est),
        name="moe_ffn",
    )(tile_e, tile_nv, xs, w1, w3, w2)


def _combine_kernel(p0_ref, p1_ref, p0n_ref, p1n_ref, ys_hbm, gate_ref, x_ref, g_ref, b_ref,
                    xo_ref, xbo_ref, buf_ref, sem):
    i = pl.program_id(0)
    n = x_ref.shape[0]
    slot = i % 2

    def row_copy(src_row, s, k, r):
        return pltpu.make_async_copy(ys_hbm.at[pl.ds(src_row, 1)], buf_ref.at[s, k, pl.ds(r, 1)], sem.at[s])

    def issue(i0_ref, i1_ref, s):
        def body(r, carry):
            row_copy(i0_ref[r], s, 0, r).start()
            row_copy(i1_ref[r], s, 1, r).start()
            return carry
        lax.fori_loop(0, n, body, 0, unroll=ROW_DMA_UNROLL)

    @pl.when(i == 0)
    def _():
        issue(p0_ref, p1_ref, 0)

    @pl.when(i + 1 < pl.num_programs(0))
    def _():
        issue(p0n_ref, p1n_ref, 1 - slot)

    def wait(r, carry):
        row_copy(0, slot, 0, r).wait()
        row_copy(0, slot, 1, r).wait()
        return carry

    lax.fori_loop(0, n, wait, 0, unroll=ROW_DMA_UNROLL)
    y = buf_ref[slot, 0] * gate_ref[:, 0:1] + buf_ref[slot, 1] * gate_ref[:, 1:2]
    z = _layer_norm(DEEPNORM_ALPHA * x_ref[...] + y, g_ref[...], b_ref[...])
    xo_ref[...] = z
    xbo_ref[...] = z.astype(BF16)


def _moe_combine(pos0, pos1, ys, gates, x, g, b):
    t = x.shape[0]
    r = ROW_DMA_TILE
    n_steps = t // r
    row = lambda wd: pl.BlockSpec((r, wd), lambda i: (i, 0))
    cur = lambda: pl.BlockSpec((r,), lambda i: (i,), memory_space=pltpu.SMEM)
    nxt = lambda: pl.BlockSpec((r,), lambda i: (jnp.minimum(i + 1, n_steps - 1),), memory_space=pltpu.SMEM)
    est = 4 * r * D_MODEL * 4 + 2 * r * D_MODEL * 10 + 2 * r * LANES * 4 + (8 << 20)
    return pl.pallas_call(
        _combine_kernel,
        grid=(n_steps,),
        in_specs=[cur(), cur(), nxt(), nxt(), pl.BlockSpec(memory_space=pl.ANY), row(LANES), row(D_MODEL),
                  _resident((1, D_MODEL)), _resident((1, D_MODEL))],
        out_specs=(row(D_MODEL), row(D_MODEL)),
        out_shape=(jax.ShapeDtypeStruct((t, D_MODEL), F32), jax.ShapeDtypeStruct((t, D_MODEL), BF16)),
        scratch_shapes=[pltpu.VMEM((2, TOP_K, r, D_MODEL), F32), pltpu.SemaphoreType.DMA((2,))],
        compiler_params=_cparams(("arbitrary",), est),
        name="moe_combine_ln",
    )(pos0, pos1, pos0, pos1, ys, gates, x, g, b)


def _moe_plan(idx, n_tiles):
    t = idx.shape[0]
    a = t * TOP_K
    e_flat = idx.reshape(a)
    onehot = (e_flat[:, None] == jnp.arange(N_EXPERTS, dtype=jnp.int32)[None, :]).astype(jnp.int32)
    csum = jnp.cumsum(onehot, axis=0)
    rank = jnp.sum(onehot * (csum - 1), axis=1)
    counts = csum[-1]
    tiles_e = (counts + MOE_TILE - 1) // MOE_TILE
    tile_end = jnp.cumsum(tiles_e)
    tile_start = tile_end - tiles_e
    dest = tile_start[e_flat] * MOE_TILE + rank
    tile_id = jnp.arange(n_tiles, dtype=jnp.int32)
    total = tile_end[-1]
    tid = jnp.minimum(tile_id, total - 1)
    tile_e = jnp.clip(jnp.searchsorted(tile_end, tid, side='right'), 0, N_EXPERTS - 1).astype(jnp.int32)
    nv = jnp.clip(counts[tile_e] - (tid - tile_start[tile_e]) * MOE_TILE, 0, MOE_TILE)
    tile_nv = jnp.where(tile_id < total, nv, 0).astype(jnp.int32)
    pos = dest.reshape(t, TOP_K).astype(jnp.int32)
    return tile_e, tile_nv, pos[:, 0], pos[:, 1]


def _moe(x, xp, router_padded, w1, w3, w2, layer, g, b):
    t = x.shape[0]
    n_tiles = (t * TOP_K) // MOE_TILE + N_EXPERTS
    idx_l, gate_l = _router(x, router_padded)
    tile_e, tile_nv, pos0, pos1 = _moe_plan(idx_l[:, :TOP_K], n_tiles)
    xs = _moe_scatter(pos0, pos1, xp, n_tiles * MOE_TILE)
    ys = _moe_ffn(tile_e, tile_nv, xs, w1, w3, w2, layer)
    return _moe_combine(pos0, pos1, ys, gate_l, x, g, b)


def _q_head_perm():
    n = np.arange(ATTN_WIDTH)
    c, half, d = n // LANES, (n % LANES) // ATTN_HEAD_DIM, n % ATTN_HEAD_DIM
    return (c + (N_Q_HEADS // N_KV_HEADS) * half) * ATTN_HEAD_DIM + d


def _ret_dim_perm():
    m = np.arange(RET_QK_DIM)
    inner = np.where(m < RET_QK_DIM // 2, 2 * m, 2 * (m - RET_QK_DIM // 2) + 1)
    return (np.arange(N_RET_HEADS)[:, None] * RET_QK_DIM + inner[None, :]).reshape(-1)


def _in_col_perm():
    perm = np.arange(IN_WIDTH)
    perm[_C_Q:_C_K] = _C_Q + _q_head_perm()
    perm[_C_RQ:_C_RK] = _C_RQ + _ret_dim_perm()
    perm[_C_RK:_C_RV] = _C_RK + _ret_dim_perm()
    return perm


def _rotary_tables(positions):
    pos = positions.astype(F32).reshape(-1)[:, None]
    half = ATTN_HEAD_DIM // 2
    inv_a = ROPE_THETA ** (-jnp.arange(0, half, dtype=F32) * 2.0 / ATTN_HEAD_DIM)
    ang = pos * inv_a
    cos, sin = jnp.cos(ang), jnp.sin(ang)
    ca = jnp.concatenate([cos, cos, cos, cos], axis=1)
    sa = jnp.concatenate([-sin, sin, -sin, sin], axis=1)
    inv_r = 1.0 / (RET_THETA ** jnp.linspace(0.0, 1.0, RET_QK_DIM // 2, dtype=F32))
    ang = pos * inv_r
    cos, sin = jnp.cos(ang), jnp.sin(ang)
    cr = jnp.concatenate([cos, cos], axis=1)
    sr = jnp.concatenate([-sin, sin], axis=1)
    return ca, sa, cr, sr


def _retention_tables():
    h = jnp.arange(N_RET_HEADS, dtype=F32)
    lg = jnp.log(1.0 - 2.0 ** (-5.0 - h))
    c = jnp.arange(RET_CHUNK, dtype=F32)
    diff = c[:, None] - c[None, :]
    decay = jnp.where(diff >= 0, jnp.exp(lg[:, None, None] * jnp.maximum(diff, 0.0)), 0.0)
    zeta = jnp.exp(lg[:, None] * (RET_CHUNK - 1.0 - c)[None])[:, :, None]
    xi = jnp.exp(lg[:, None] * (c + 1.0)[None])[:, :, None]
    g_chunk = jnp.exp(lg * RET_CHUNK)
    return decay, zeta, xi, g_chunk


def kernel(x, mem, positions, ln_g, ln_b, w_in, sinks, w_o, xq, xk, xv, xo,
           ffn_w1, ffn_w3, ffn_w2, router, moe_w1, moe_w3, moe_w2):
    batch, seq, _ = x.shape
    mem_len = mem.shape[1]
    t = batch * seq
    ca, sa, cr, sr = _rotary_tables(positions)
    decay, zeta, xi, g_chunk = _retention_tables()
    in_perm = _in_col_perm()
    out_perm = np.concatenate([_q_head_perm(), np.arange(ATTN_WIDTH, ATTN_WIDTH + RET_V_WIDTH)])

    xf = x.reshape(t, D_MODEL)
    xb = xf.astype(BF16)
    mem_b = mem.reshape(batch * mem_len, D_MODEL).astype(BF16)
    for l in range(DEPTH):
        gl = ln_g[l].reshape(3, 1, D_MODEL)
        bl = ln_b[l].reshape(3, 1, D_MODEL)
        w_in_l = w_in[l][:, in_perm].astype(BF16)
        w_o_l = w_o[l][out_perm, :].astype(BF16)
        q, k, v, rq, rk, rv, rg = _in_proj(xb, w_in_l, ca, sa, cr, sr)
        attn = _swa_attention(q, k, v, sinks[l], batch, seq)
        ret = _retention(rq, rk, rv, rg, decay, zeta, xi, g_chunk, batch, seq)
        xf, xb = _out_proj(attn, ret, xf, w_o_l, gl[0], bl[0])
        kmem = _matmul_bf16(mem_b, xk[l].astype(BF16))
        vmem = _matmul_bf16(mem_b, xv[l].astype(BF16))
        is_moe = l % 2 == 1
        xf, xb = _xattn(xb, xf, kmem, vmem, xq[l].astype(BF16), xo[l].astype(BF16),
                        gl[1], bl[1], seq, mem_len, packed=is_moe)
        j = l // 2
        if is_moe:
            router_padded = jnp.pad(router[j], ((0, 0), (0, LANES - N_EXPERTS)))
            xf, xb = _moe(xf, xb, router_padded, moe_w1, moe_w3, moe_w2, j, gl[2], bl[2])
        else:
            xf, xb = _ffn(xb, xf, ffn_w1[j].astype(BF16), ffn_w3[j].astype(BF16),
                          ffn_w2[j].astype(BF16), gl[2], bl[2])
    return xf.reshape(batch, seq, D_MODEL)
```
